```python
import jax, jax.numpy as jnp
from jax import lax
import numpy as np

D_MODEL = 4096
BATCH = 2
SEQ = 8192
DEPTH = 2

N_MEM = 256
D_MIX = D_MODEL
RWKV_HEAD_DIM = 64
D_RWKV = 3 * D_MIX // 8
RWKV_HEADS = D_RWKV // RWKV_HEAD_DIM
DECAY_LORA = 64
AAA_LORA = 64
GATE_LORA = 224
D_CONV = D_MIX // 4
CONV_K = 31
HGRN_HEAD_DIM = 128
D_HGRN = D_MIX - D_RWKV - D_CONV
HGRN_HEADS = D_HGRN // HGRN_HEAD_DIM
HGRN_CHUNK = 64
MIN_FORGET = 1e-30
XATTN_HEADS = 4
XATTN_HEAD_DIM = 128
D_XATTN = XATTN_HEADS * XATTN_HEAD_DIM
N_GROUPS = 4
EXPERTS_PER_GROUP = 8
N_EXPERTS = N_GROUPS * EXPERTS_PER_GROUP
TOP_K = 2
D_EXPERT = 768

A_COLS = 3 * D_RWKV + DECAY_LORA + AAA_LORA + GATE_LORA
B_COLS = 2 * D_CONV
C_COLS = 4 * D_HGRN
D_IN = A_COLS + B_COLS + C_COLS

RMS_EPS = 1e-6
LN_EPS = 1e-5
GN_EPS = 64e-5

kernel_name = "hymba_rwkv7_conformer_hgrn2_hmoe"


def rms_norm(x, w):
    xf = x.astype(jnp.float32)
    y = xf * lax.rsqrt(jnp.mean(xf * xf, axis=-1, keepdims=True) + RMS_EPS)
    return (y * w.astype(jnp.float32)).astype(x.dtype)


def layer_norm(x, w, b):
    xf = x.astype(jnp.float32)
    mu = jnp.mean(xf, axis=-1, keepdims=True)
    var = jnp.mean(jnp.square(xf - mu), axis=-1, keepdims=True)
    y = (xf - mu) * lax.rsqrt(var + LN_EPS) * w.astype(jnp.float32) + b.astype(jnp.float32)
    return y.astype(x.dtype)


def token_shift(x):
    return jnp.pad(x, ((0, 0), (1, 0), (0, 0)))[:, :-1]


def rwkv7_scan(r, w, k, v, kk, a):
    B, _, H, N = r.shape

    def step(state, inp):
        r_t, w_t, k_t, v_t, kk_t, a_t = inp
        sa = jnp.einsum('bhvk,bhk->bhv', state, -kk_t)
        state = (state * w_t[:, :, None, :]
                 + sa[..., None] * (kk_t * a_t)[:, :, None, :]
                 + v_t[..., None] * k_t[:, :, None, :])
        return state, jnp.einsum('bhvk,bhk->bhv', state, r_t)

    xs = tuple(jnp.moveaxis(t, 1, 0) for t in (r, w, k, v, kk, a))
    s0 = jnp.zeros((B, H, N, N), jnp.float32)
    _, y = lax.scan(step, s0, xs)
    return jnp.moveaxis(y, 0, 1)


def rwkv7_mixer(p_a, mu, w_up, w0, a_up, a0, g_up, k_k, k_a, r_k, lnx_w, lnx_b):
    B, S, _ = p_a.shape
    H, N = RWKV_HEADS, RWKV_HEAD_DIM
    p = p_a.astype(jnp.float32)
    p = p + (token_shift(p) - p) * mu.astype(jnp.float32)
    r, k, v, xw, xa, xg = jnp.split(
        p, [D_RWKV, 2 * D_RWKV, 3 * D_RWKV, 3 * D_RWKV + DECAY_LORA,
            3 * D_RWKV + DECAY_LORA + AAA_LORA], axis=-1)
    f32 = lambda t: t.astype(jnp.float32)
    w_log = -jax.nn.softplus(-(f32(w0) + jnp.tanh(xw) @ f32(w_up))) - 0.5
    decay = jnp.exp(-jnp.exp(w_log))
    a = jax.nn.sigmoid(f32(a0) + xa @ f32(a_up))
    g = jax.nn.sigmoid(xg) @ f32(g_up)
    heads = lambda t: t.reshape(B, S, H, N)
    kk = heads(k * f32(k_k))
    kk = kk / jnp.maximum(jnp.sqrt(jnp.sum(kk * kk, axis=-1, keepdims=True)), 1e-12)
    k = k * (1.0 + (a - 1.0) * f32(k_a))
    rh, kh, vh, ah, wh = heads(r), heads(k), heads(v), heads(a), heads(decay)
    y = rwkv7_scan(rh, wh, kh, vh, kk, ah)
    mean = jnp.mean(y, axis=-1, keepdims=True)
    var = jnp.mean(jnp.square(y - mean), axis=-1, keepdims=True)
    y = (y - mean) * lax.rsqrt(var + GN_EPS) * f32(lnx_w).reshape(H, N) + f32(lnx_b).reshape(H, N)
    bonus = jnp.sum(rh * kh * f32(r_k), axis=-1, keepdims=True) * vh
    out = (y + bonus).reshape(B, S, D_RWKV) * g
    return out.astype(p_a.dtype)


def conformer_conv(p_b, conv_w, conv_b, ln_w, ln_b, pw2, pw2_b):
    val, gate = jnp.split(p_b, 2, axis=-1)
    u = val * jax.nn.sigmoid(gate)
    u = lax.conv_general_dilated(
        u, conv_w[:, None, :].astype(u.dtype), window_strides=(1,),
        padding=[(CONV_K - 1, 0)], dimension_numbers=('NWC', 'WIO', 'NWC'),
        feature_group_count=D_CONV) + conv_b
    u = jax.nn.silu(layer_norm(u, ln_w, ln_b))
    return u @ pw2 + pw2_b


def hgrn2_chunk_scan(q, k, v, log_f):
    B, S, H, K = q.shape
    V = v.shape[-1]
    C = HGRN_CHUNK
    n = S // C
    to_chunks = lambda t: t.reshape(B, n, C, H, t.shape[-1]).transpose(1, 0, 3, 2, 4)
    causal = jnp.tril(jnp.ones((C, C), dtype=bool))[:, :, None]

    def step(state, inp):
        q_c, k_c, v_c, lf_c = inp
        G = jnp.cumsum(lf_c, axis=2)
        G_last = G[:, :, -1:, :]
        inter = jnp.einsum('bhtk,bhkv->bhtv', q_c * jnp.exp(G), state)
        diff = jnp.minimum(G[:, :, :, None, :] - G[:, :, None, :, :], 0.0)
        decay = jnp.where(causal, jnp.exp(diff), 0.0)
        scores = jnp.einsum('bhtk,bhsk,bhtsk->bhts', q_c, k_c, decay)
        intra = jnp.einsum('bhts,bhsv->bhtv', scores, v_c)
        state = (jnp.exp(G_last[:, :, 0, :])[..., None] * state
                 + jnp.einsum('bhsk,bhsv->bhkv', k_c * jnp.exp(G_last - G), v_c))
        return state, inter + intra

    s0 = jnp.zeros((B, H, K, V), jnp.float32)
    _, o = lax.scan(step, s0, tuple(to_chunks(t) for t in (q, k, v, log_f)))
    return o.transpose(1, 0, 3, 2, 4).reshape(B, S, H, V)


def hgrn2_mixer(p_c, lb, norm_w):
    B, S, _ = p_c.shape
    H, Dh = HGRN_HEADS, HGRN_HEAD_DIM
    pf = p_c.astype(jnp.float32)
    q, z, i, g = jnp.split(pf, 4, axis=-1)
    lb = lb.astype(jnp.float32)
    f = lb + (1.0 - lb) * jax.nn.sigmoid(z)
    log_f = jnp.log(jnp.maximum(f, MIN_FORGET))
    k = (1.0 - lb) * jax.nn.sigmoid(-z)
    heads = lambda t: t.reshape(B, S, H, Dh)
    o = hgrn2_chunk_scan(heads(jax.nn.silu(q)), heads(k), heads(i), heads(log_f))
    o = o * lax.rsqrt(jnp.mean(o * o, axis=-1, keepdims=True) + RMS_EPS)
    o = o * norm_w.astype(jnp.float32).reshape(H, Dh)
    return (o.reshape(B, S, D_HGRN) * jax.nn.silu(g)).astype(p_c.dtype)


def memory_cross_attention(xn, mem_n, wq, wk, wv, wo):
    B, S, _ = xn.shape
    M = mem_n.shape[1]
    q = (xn @ wq).reshape(B, S, XATTN_HEADS, XATTN_HEAD_DIM)
    k = (mem_n @ wk).reshape(B, M, XATTN_HEADS, XATTN_HEAD_DIM)
    v = (mem_n @ wv).reshape(B, M, XATTN_HEADS, XATTN_HEAD_DIM)
    scores = jnp.einsum('bshd,bmhd->bhsm', q, k).astype(jnp.float32) * (XATTN_HEAD_DIM ** -0.5)
    probs = jax.nn.softmax(scores, axis=-1).astype(v.dtype)
    o = jnp.einsum('bhsm,bmhd->bshd', probs, v).reshape(B, S, D_XATTN)
    return o @ wo


def hierarchical_moe(xn, router_g, router_g_b, router_e, router_e_b, w1, w3, w2):
    B, S, D = xn.shape
    xt = xn.reshape(B * S, D)
    lg1 = (xt @ router_g + router_g_b).astype(jnp.float32)
    p1 = jax.nn.softmax(lg1, axis=-1)
    grp = jnp.argmax(lg1, axis=-1)
    gate1 = jnp.take_along_axis(p1, grp[:, None], axis=-1)
    lg2_all = jnp.einsum('td,gde->tge', xt, router_e) + router_e_b
    lg2 = jnp.take_along_axis(lg2_all, grp[:, None, None], axis=1)[:, 0].astype(jnp.float32)
    p2 = jax.nn.softmax(lg2, axis=-1)
    top_p, top_i = lax.top_k(p2, TOP_K)
    top_p = top_p / jnp.sum(top_p, axis=-1, keepdims=True)
    weights = gate1 * top_p
    eid = grp[:, None] * EXPERTS_PER_GROUP + top_i
    combine = jnp.sum(jax.nn.one_hot(eid, N_EXPERTS, dtype=jnp.float32) * weights[..., None], axis=1)
    combine = combine.astype(xt.dtype)
    y = jnp.zeros_like(xt)
    for e in range(N_EXPERTS):
        h = jax.nn.silu(xt @ w1[e]) * (xt @ w3[e])
        y = y + (h * combine[:, e:e + 1]) @ w2[e]
    return y.reshape(B, S, D)


def setup_inputs(seed: int = 0) -> dict:
    key = jax.random.key(seed)
    keys = iter(jax.random.split(key, 64))
    nk = lambda: next(keys)
    L, D = DEPTH, D_MODEL
    normal = lambda shape, scale: jax.random.normal(nk(), shape, jnp.float32) * scale
    gain = lambda shape: 1.0 + normal(shape, 0.02)
    return {
        "x": normal((BATCH, SEQ, D), 1.0),
        "mem": normal((BATCH, N_MEM, D), 1.0),
        "norm_mix_w": gain((L, D)),
        "w_in": normal((L, D, D_IN), D ** -0.5),
        "rwkv_mu": jax.random.uniform(nk(), (L, A_COLS), jnp.float32),
        "rwkv_w_up": normal((L, DECAY_LORA, D_RWKV), 0.1),
        "rwkv_w0": jax.random.uniform(nk(), (L, D_RWKV), jnp.float32, -5.0, 0.0),
        "rwkv_a_up": normal((L, AAA_LORA, D_RWKV), 0.1),
        "rwkv_a0": normal((L, D_RWKV), 0.1),
        "rwkv_g_up": normal((L, GATE_LORA, D_RWKV), GATE_LORA ** -0.5),
        "rwkv_k_k": 0.85 + normal((L, D_RWKV), 0.02),
        "rwkv_k_a": gain((L, D_RWKV)),
        "rwkv_r_k": normal((L, RWKV_HEADS, RWKV_HEAD_DIM), 0.1),
        "rwkv_lnx_w": gain((L, D_RWKV)),
        "rwkv_lnx_b": normal((L, D_RWKV), 0.01),
        "conv_w": normal((L, CONV_K, D_CONV), CONV_K ** -0.5),
        "conv_b": normal((L, D_CONV), 0.01),
        "conv_ln_w": gain((L, D_CONV)),
        "conv_ln_b": normal((L, D_CONV), 0.01),
        "conv_pw2": normal((L, D_CONV, D_CONV), D_CONV ** -0.5),
        "conv_pw2_b": normal((L, D_CONV), 0.01),
        "hgrn_lb_logits": normal((L, D_HGRN), 1.0),
        "hgrn_norm_w": gain((L, D_HGRN)),
        "w_out": normal((L, D_MIX, D), D_MIX ** -0.5),
        "norm_xattn_w": gain((L, D)),
        "mem_norm_w": gain((D,)),
        "xattn_wq": normal((L, D, D_XATTN), D ** -0.5),
        "xattn_wk": normal((L, D, D_XATTN), D ** -0.5),
        "xattn_wv": normal((L, D, D_XATTN), D ** -0.5),
        "xattn_wo": normal((L, D_XATTN, D), D_XATTN ** -0.5),
        "norm_ffn_w": gain((L, D)),
        "router_g": normal((L, D, N_GROUPS), D ** -0.5),
        "router_g_b": normal((L, N_GROUPS), 0.01),
        "router_e": normal((L, N_GROUPS, D, EXPERTS_PER_GROUP), D ** -0.5),
        "router_e_b": normal((L, N_GROUPS, EXPERTS_PER_GROUP), 0.01),
        "expert_w1": normal((L, N_EXPERTS, D, D_EXPERT), D ** -0.5),
        "expert_w3": normal((L, N_EXPERTS, D, D_EXPERT), D ** -0.5),
        "expert_w2": normal((L, N_EXPERTS, D_EXPERT, D), D_EXPERT ** -0.5),
        "final_norm_w": gain((D,)),
    }


def reference(x, mem, norm_mix_w, w_in, rwkv_mu, rwkv_w_up, rwkv_w0, rwkv_a_up, rwkv_a0,
              rwkv_g_up, rwkv_k_k, rwkv_k_a, rwkv_r_k, rwkv_lnx_w, rwkv_lnx_b,
              conv_w, conv_b, conv_ln_w, conv_ln_b, conv_pw2, conv_pw2_b,
              hgrn_lb_logits, hgrn_norm_w, w_out, norm_xattn_w, mem_norm_w,
              xattn_wq, xattn_wk, xattn_wv, xattn_wo, norm_ffn_w,
              router_g, router_g_b, router_e, router_e_b,
              expert_w1, expert_w3, expert_w2, final_norm_w):
    sm = jax.nn.softmax(hgrn_lb_logits.astype(jnp.float32), axis=0)
    lower_bounds = jnp.cumsum(sm, axis=0) - sm[0:1]
    mem_n = rms_norm(mem, mem_norm_w)
    for l in range(DEPTH):
        h = rms_norm(x, norm_mix_w[l])
        p = h @ w_in[l]
        p_a, p_b, p_c = p[..., :A_COLS], p[..., A_COLS:A_COLS + B_COLS], p[..., A_COLS + B_COLS:]
        y_a = rwkv7_mixer(p_a, rwkv_mu[l], rwkv_w_up[l], rwkv_w0[l], rwkv_a_up[l], rwkv_a0[l],
                          rwkv_g_up[l], rwkv_k_k[l], rwkv_k_a[l], rwkv_r_k[l],
                          rwkv_lnx_w[l], rwkv_lnx_b[l])
        y_b = conformer_conv(p_b, conv_w[l], conv_b[l], conv_ln_w[l], conv_ln_b[l],
                             conv_pw2[l], conv_pw2_b[l])
        y_c = hgrn2_mixer(p_c, lower_bounds[l], hgrn_norm_w[l])
        x = x + jnp.concatenate([y_a, y_b, y_c], axis=-1) @ w_out[l]
        x = x + memory_cross_attention(rms_norm(x, norm_xattn_w[l]), mem_n, xattn_wq[l],
                                       xattn_wk[l], xattn_wv[l], xattn_wo[l])
        x = x + hierarchical_moe(rms_norm(x, norm_ffn_w[l]), router_g[l], router_g_b[l],
                                 router_e[l], router_e_b[l], expert_w1[l], expert_w3[l],
                                 expert_w2[l])
    return rms_norm(x, final_norm_w)
```

```python
import functools

import jax
import jax.numpy as jnp
from jax import lax
from jax.experimental import pallas as pl
from jax.experimental.pallas import tpu as pltpu

F32 = jnp.float32
BF16 = jnp.bfloat16
HI = lax.Precision.HIGHEST

RMS_EPS = 1e-6
LN_EPS = 1e-5
GN_EPS = 64e-5
MIN_FORGET = 1e-30

RWKV_N = 64
HGRN_N = 128
CHUNK = 64
SUB = 16
CONV_K = 31
CONV_HALO = 32
N_GROUPS = 4
EXPERTS_PER_GROUP = 8
N_EXPERTS = N_GROUPS * EXPERTS_PER_GROUP
XATTN_HEADS = 4
LANE = 128

VMEM_BIG = 56 * 1024 * 1024


def _params(sem, vmem=None):
    return pltpu.CompilerParams(dimension_semantics=sem, vmem_limit_bytes=vmem)


def _nt(a, b, precision=None):
    return lax.dot_general(a, b, (((1,), (1,)), ((), ())), precision=precision,
                           preferred_element_type=F32)


def _tn(a, b, precision=None):
    return lax.dot_general(a, b, (((0,), (0,)), ((), ())), precision=precision,
                           preferred_element_type=F32)


def _mm(a, b, precision=None):
    return jnp.dot(a, b, precision=precision, preferred_element_type=F32)


def _sigmoid(x):
    return 1.0 / (1.0 + jnp.exp(-x))


def _rmsnorm_body(x_ref, w_ref, o_ref):
    x = x_ref[...].astype(F32)
    ms = jnp.mean(x * x, axis=-1, keepdims=True)
    o_ref[...] = (x * lax.rsqrt(ms + RMS_EPS) * w_ref[...]).astype(o_ref.dtype)


def _rmsnorm(x, w, out_dtype, tm=256):
    m, d = x.shape
    tm = min(tm, m)
    return pl.pallas_call(
        _rmsnorm_body,
        grid=(m // tm,),
        in_specs=[pl.BlockSpec((tm, d), lambda i: (i, 0)), pl.BlockSpec((1, d), lambda i: (0, 0))],
        out_specs=pl.BlockSpec((tm, d), lambda i: (i, 0)),
        out_shape=jax.ShapeDtypeStruct((m, d), out_dtype),
        compiler_params=_params(("parallel",), VMEM_BIG),
        name="rmsnorm",
    )(x, w.reshape(1, d).astype(F32))


def _add3_rmsnorm_body(x_ref, a_ref, b_ref, w_ref, s_ref, o_ref):
    x = x_ref[...] + a_ref[...] + b_ref[...]
    s_ref[...] = x
    ms = jnp.mean(x * x, axis=-1, keepdims=True)
    o_ref[...] = (x * lax.rsqrt(ms + RMS_EPS) * w_ref[...]).astype(o_ref.dtype)


def _add3_rmsnorm(x, a, b, w, out_dtype, tm=128):
    m, d = x.shape
    tm = min(tm, m)
    row = pl.BlockSpec((tm, d), lambda i: (i, 0))
    return pl.pallas_call(
        _add3_rmsnorm_body,
        grid=(m // tm,),
        in_specs=[row, row, row, pl.BlockSpec((1, d), lambda i: (0, 0))],
        out_specs=[row, row],
        out_shape=[jax.ShapeDtypeStruct((m, d), F32), jax.ShapeDtypeStruct((m, d), out_dtype)],
        compiler_params=_params(("parallel",), VMEM_BIG),
        name="add3_rmsnorm",
    )(x, a, b, w.reshape(1, d).astype(F32))


def _matmul_body(*refs, n_pairs, has_bias, has_res):
    a_refs = refs[:n_pairs]
    b_refs = refs[n_pairs:2 * n_pairs]
    pos = 2 * n_pairs
    acc = _mm(a_refs[0][...], b_refs[0][...])
    for a_ref, b_ref in zip(a_refs[1:], b_refs[1:]):
        acc = acc + _mm(a_ref[...], b_ref[...])
    if has_bias:
        acc = acc + refs[pos][...]
        pos += 1
    if has_res:
        acc = acc + refs[pos][...]
        pos += 1
    o_ref = refs[pos]
    o_ref[...] = acc.astype(o_ref.dtype)


def _matmul(a_list, b_list, *, bias=None, residual=None, out_dtype=F32, tm=1024, tn=512):
    m = a_list[0].shape[0]
    n = b_list[0].shape[1]
    tm = min(tm, m)
    tn = min(tn, n)
    assert m % tm == 0 and n % tn == 0, (m, n, tm, tn)
    in_specs = [pl.BlockSpec((tm, a.shape[1]), lambda i, j: (i, 0)) for a in a_list]
    in_specs += [pl.BlockSpec((b.shape[0], tn), lambda i, j: (0, j)) for b in b_list]
    args = list(a_list) + list(b_list)
    if bias is not None:
        in_specs.append(pl.BlockSpec((1, tn), lambda i, j: (0, j)))
        args.append(bias.reshape(1, n).astype(F32))
    if residual is not None:
        in_specs.append(pl.BlockSpec((tm, tn), lambda i, j: (i, j)))
        args.append(residual)
    body = functools.partial(_matmul_body, n_pairs=len(a_list), has_bias=bias is not None,
                             has_res=residual is not None)
    return pl.pallas_call(
        body,
        grid=(m // tm, n // tn),
        in_specs=in_specs,
        out_specs=pl.BlockSpec((tm, tn), lambda i, j: (i, j)),
        out_shape=jax.ShapeDtypeStruct((m, n), out_dtype),
        compiler_params=_params(("parallel", "parallel"), VMEM_BIG),
        name="matmul",
    )(*args)


def _rwkv_prep_body(p_ref, mu_ref, wup_ref, w0_ref, aup_ref, a0_ref, gup_ref, kk_ref, ka_ref,
                    rk_ref, e_ref, et_ref,
                    rt_ref, kt_ref, bt_ref, at_ref, v_ref, gc_ref, g_ref, bonus_ref,
                    carry_ref, *, d_r, ts):
    j = pl.program_id(1)

    @pl.when(j == 0)
    def _():
        carry_ref[...] = jnp.zeros_like(carry_ref)

    p = p_ref[0]
    rows = lax.broadcasted_iota(jnp.int32, p.shape, 0)
    prev = jnp.where(rows == 0, carry_ref[0:1, :], pltpu.roll(p, 1, axis=0))
    carry_ref[0:1, :] = p[ts - 1:ts, :]
    pm = p + (prev - p) * mu_ref[...]

    r = pm[:, 0:d_r]
    k = pm[:, d_r:2 * d_r]
    v = pm[:, 2 * d_r:3 * d_r]
    c0 = 3 * d_r
    xw = pm[:, c0:c0 + LANE]
    xa = pm[:, c0 + LANE:c0 + 2 * LANE]
    xg = pm[:, c0 + 2 * LANE:c0 + 4 * LANE]

    zw = w0_ref[...] + _mm(jnp.tanh(xw), wup_ref[...], HI)
    w_log = -(jnp.maximum(-zw, 0.0) + jnp.log(1.0 + jnp.exp(-jnp.abs(zw)))) - 0.5
    lw = -jnp.exp(w_log)
    a = _sigmoid(a0_ref[...] + _mm(xa, aup_ref[...], HI))
    g = _mm(_sigmoid(xg), gup_ref[...], HI)

    e = e_ref[...]
    et = et_ref[...]
    kk = k * kk_ref[...]
    ss = _mm(kk * kk, e, HI)
    inv = 1.0 / jnp.maximum(jnp.sqrt(ss), 1e-12)
    kk = kk * _mm(inv, et, HI)
    kmod = k * (1.0 + (a - 1.0) * ka_ref[...])

    bonus = _mm(_mm(r * kmod * rk_ref[...], e, HI), et, HI) * v

    ri = lax.broadcasted_iota(jnp.int32, (ts, ts), 0)
    ci = lax.broadcasted_iota(jnp.int32, (ts, ts), 1)
    ltri = jnp.where((ri >= ci) & ((ri // CHUNK) == (ci // CHUNK)), 1.0, 0.0).astype(F32)
    gcum = _mm(ltri, lw, HI)
    eg = jnp.exp(gcum)
    eng = jnp.exp(-gcum)

    rt_ref[0] = r * eg
    kt_ref[0] = kmod * eng
    bt_ref[0] = kk * a * eng
    at_ref[0] = -kk * jnp.exp(gcum - lw)
    v_ref[0] = v
    g_ref[0] = g
    bonus_ref[0] = bonus
    gc_ref[0, 0] = jnp.zeros(gc_ref.shape[2:], F32)
    for c in range(ts // CHUNK):
        gc_ref[0, 0, c:c + 1, :] = eg[(c + 1) * CHUNK - 1:(c + 1) * CHUNK, :]


def _rwkv_prep(p_a, mu, wup, w0, aup, a0, gup, k_k, k_a, r_k, *, d_r, ts):
    b, s, cols = p_a.shape
    nh = d_r // RWKV_N
    head = jnp.arange(d_r, dtype=jnp.int32) // RWKV_N
    e = (head[:, None] == jnp.arange(LANE, dtype=jnp.int32)[None, :]).astype(F32)
    et = e.T
    row = lambda a: a.reshape(1, -1).astype(F32)
    full = lambda shape: pl.BlockSpec(shape, lambda i, j: (0,) * len(shape))
    seq = lambda w: pl.BlockSpec((1, ts, w), lambda i, j: (i, j, 0))
    out_seq = jax.ShapeDtypeStruct((b, s, d_r), F32)
    body = functools.partial(_rwkv_prep_body, d_r=d_r, ts=ts)
    del nh
    return pl.pallas_call(
        body,
        grid=(b, s // ts),
        in_specs=[seq(cols), full((1, cols)), full(wup.shape), full((1, d_r)), full(aup.shape),
                  full((1, d_r)), full(gup.shape), full((1, d_r)), full((1, d_r)), full((1, d_r)),
                  full(e.shape), full(et.shape)],
        out_specs=[seq(d_r), seq(d_r), seq(d_r), seq(d_r), seq(d_r),
                   pl.BlockSpec((1, 1, 8, d_r), lambda i, j: (i, j, 0, 0)),
                   seq(d_r), seq(d_r)],
        out_shape=[out_seq, out_seq, out_seq, out_seq, out_seq,
                   jax.ShapeDtypeStruct((b, s // ts, 8, d_r), F32), out_seq, out_seq],
        scratch_shapes=[pltpu.VMEM((8, cols), F32)],
        compiler_params=_params(("arbitrary", "arbitrary"), VMEM_BIG),
        name="rwkv_prep",
    )(p_a, row(mu), wup, row(w0), aup, row(a0), gup, row(k_k), row(k_a), row(r_k), e, et)


def _rwkv_scan_body(rt_ref, kt_ref, bt_ref, at_ref, v_ref, gc_ref, g_ref, bonus_ref,
                    lnw_ref, lnb_ref, o_ref, state_ref, *, hb, nchunk):
    @pl.when(pl.program_id(2) == 0)
    def _():
        state_ref[...] = jnp.zeros_like(state_ref)

    ri = lax.broadcasted_iota(jnp.int32, (CHUNK, CHUNK), 0)
    ci = lax.broadcasted_iota(jnp.int32, (CHUNK, CHUNK), 1)
    strict = ri > ci
    incl = ri >= ci
    diag_blk = (ri // SUB) == (ci // SUB)
    eye = jnp.where(ri == ci, 1.0, 0.0).astype(F32)

    for c in range(nchunk):
        rs = slice(c * CHUNK, (c + 1) * CHUNK)
        for h in range(hb):
            ls = slice(h * RWKV_N, (h + 1) * RWKV_N)
            rt = rt_ref[0, rs, ls]
            kt = kt_ref[0, rs, ls]
            bt = bt_ref[0, rs, ls]
            at = at_ref[0, rs, ls]
            v = v_ref[0, rs, ls]
            gam = gc_ref[0, 0, c:c + 1, ls]
            s0 = state_ref[h]

            a_ab = jnp.where(strict, _nt(at, bt, HI), 0.0)
            a_ak = jnp.where(strict, _nt(at, kt, HI), 0.0)
            a_rb = jnp.where(incl, _nt(rt, bt, HI), 0.0)
            a_rk = jnp.where(incl, _nt(rt, kt, HI), 0.0)

            x = _nt(at, s0, HI) + _mm(a_ak, v, HI)

            a_d = jnp.where(diag_blk, a_ab, 0.0)
            a_o = a_ab - a_d
            pinv = eye + a_d
            a2 = _mm(a_d, a_d, HI)
            pinv = pinv + _mm(a2, pinv, HI)
            a4 = _mm(a2, a2, HI)
            pinv = pinv + _mm(a4, pinv, HI)
            a8 = _mm(a4, a4, HI)
            pinv = pinv + _mm(a8, pinv, HI)
            m1 = _mm(pinv, a_o, HI)
            m2 = _mm(m1, m1, HI)
            u = _mm(pinv, x, HI)
            u = u + _mm(m2, u, HI)
            u = u + _mm(m1, u, HI)

            y = _nt(rt, s0, HI) + _mm(a_rb, u, HI) + _mm(a_rk, v, HI)
            state_ref[h] = (s0 + _tn(u, bt, HI) + _tn(v, kt, HI)) * gam

            mean = jnp.mean(y, axis=-1, keepdims=True)
            yc = y - mean
            var = jnp.mean(yc * yc, axis=-1, keepdims=True)
            yn = yc * lax.rsqrt(var + GN_EPS) * lnw_ref[0:1, ls] + lnb_ref[0:1, ls]
            o_ref[0, rs, ls] = ((yn + bonus_ref[0, rs, ls]) * g_ref[0, rs, ls]).astype(o_ref.dtype)


def _rwkv_scan(rt, kt, bt, at, v, gc, g, bonus, lnw, lnb, *, ts, hb=2):
    b, s, d_r = rt.shape
    w = hb * RWKV_N
    seq = pl.BlockSpec((1, ts, w), lambda i, h, j: (i, j, h))
    par = pl.BlockSpec((1, w), lambda i, h, j: (0, h))
    body = functools.partial(_rwkv_scan_body, hb=hb, nchunk=ts // CHUNK)
    return pl.pallas_call(
        body,
        grid=(b, d_r // w, s // ts),
        in_specs=[seq, seq, seq, seq, seq,
                  pl.BlockSpec((1, 1, 8, w), lambda i, h, j: (i, j, 0, h)),
                  seq, seq, par, par],
        out_specs=seq,
        out_shape=jax.ShapeDtypeStruct((b, s, d_r), BF16),
        scratch_shapes=[pltpu.VMEM((hb, RWKV_N, RWKV_N), F32)],
        compiler_params=_params(("parallel", "parallel", "arbitrary"), VMEM_BIG),
        name="rwkv_scan",
    )(rt, kt, bt, at, v, gc, g, bonus, lnw.reshape(1, d_r).astype(F32), lnb.reshape(1, d_r).astype(F32))


def _hgrn_body(q_ref, z_ref, i_ref, g_ref, lb_ref, nw_ref, o_ref, state_ref, *, nchunk):
    @pl.when(pl.program_id(2) == 0)
    def _():
        state_ref[...] = jnp.zeros_like(state_ref)

    ri = lax.broadcasted_iota(jnp.int32, (CHUNK, CHUNK), 0)
    ci = lax.broadcasted_iota(jnp.int32, (CHUNK, CHUNK), 1)
    ltri = jnp.where(ri >= ci, 1.0, 0.0).astype(F32)
    rs_ = lax.broadcasted_iota(jnp.int32, (SUB, SUB), 0)
    cs_ = lax.broadcasted_iota(jnp.int32, (SUB, SUB), 1)
    sub_incl = rs_ >= cs_
    lb = lb_ref[...]

    for c in range(nchunk):
        rows = slice(c * CHUNK, (c + 1) * CHUNK)
        q = q_ref[0, rows, :]
        z = z_ref[0, rows, :]
        v = i_ref[0, rows, :]
        g = g_ref[0, rows, :]
        f = lb + (1.0 - lb) * _sigmoid(z)
        lf = jnp.log(jnp.maximum(f, MIN_FORGET))
        k = (1.0 - lb) * (1.0 / (1.0 + jnp.exp(z)))
        qs = q * _sigmoid(q)
        gc = _mm(ltri, lf, HI)
        st = state_ref[...]

        inter = _nt(qs * jnp.exp(gc), st, HI)
        outs = []
        for i in range(CHUNK // SUB):
            lo = i * SUB
            gi = gc[lo:lo + SUB]
            qi = qs[lo:lo + SUB]
            ki = k[lo:lo + SUB]
            vi = v[lo:lo + SUB]
            diff = gi[:, None, :] - gi[None, :, :]
            w3 = jnp.exp(jnp.minimum(diff, 0.0))
            d = jnp.sum(qi[:, None, :] * ki[None, :, :] * w3, axis=-1)
            d = jnp.where(sub_incl, d, 0.0)
            oi = _mm(d, vi, HI)
            if i > 0:
                gref = gc[lo - 1:lo]
                qr = qi * jnp.exp(gi - gref)
                kp = k[:lo] * jnp.exp(gref - gc[:lo])
                oi = oi + _mm(_nt(qr, kp, HI), v[:lo], HI)
            outs.append(oi)
        o = inter + jnp.concatenate(outs, axis=0)

        glast = gc[CHUNK - 1:CHUNK]
        kd = k * jnp.exp(glast - gc)
        state_ref[...] = st * jnp.exp(glast) + _tn(v, kd, HI)

        o = o * lax.rsqrt(jnp.mean(o * o, axis=-1, keepdims=True) + RMS_EPS) * nw_ref[...]
        o_ref[0, rows, :] = (o * (g * _sigmoid(g))).astype(o_ref.dtype)


def _hgrn(p_c, lb, norm_w, *, ts):
    b, s, cols = p_c.shape
    d_h = cols // 4
    nh = d_h // HGRN_N
    part = lambda k: pl.BlockSpec((1, ts, HGRN_N), lambda i, h, j, k=k: (i, j, k * nh + h))
    par = pl.BlockSpec((1, HGRN_N), lambda i, h, j: (0, h))
    body = functools.partial(_hgrn_body, nchunk=ts // CHUNK)
    return pl.pallas_call(
        body,
        grid=(b, nh, s // ts),
        in_specs=[part(0), part(1), part(2), part(3), par, par],
        out_specs=pl.BlockSpec((1, ts, HGRN_N), lambda i, h, j: (i, j, h)),
        out_shape=jax.ShapeDtypeStruct((b, s, d_h), BF16),
        scratch_shapes=[pltpu.VMEM((HGRN_N, HGRN_N), F32)],
        compiler_params=_params(("parallel", "parallel", "arbitrary"), VMEM_BIG),
        name="hgrn_scan",
    )(p_c, p_c, p_c, p_c, lb.reshape(1, d_h).astype(F32), norm_w.reshape(1, d_h).astype(F32))


def _conv_body(val_ref, gate_ref, cw_ref, cb_ref, lnw_ref, lnb_ref, o_ref, ubuf_ref, *, ts):
    j = pl.program_id(1)

    @pl.when(j == 0)
    def _():
        ubuf_ref[0:CONV_HALO, :] = jnp.zeros((CONV_HALO, ubuf_ref.shape[1]), F32)

    @pl.when(j > 0)
    def _():
        ubuf_ref[0:CONV_HALO, :] = ubuf_ref[ts:ts + CONV_HALO, :]

    ubuf_ref[CONV_HALO:CONV_HALO + ts, :] = val_ref[0] * _sigmoid(gate_ref[0])
    off = CONV_HALO - (CONV_K - 1)
    acc = cw_ref[0:1, :] * ubuf_ref[off:off + ts, :]
    for t in range(1, CONV_K):
        acc = acc + cw_ref[t:t + 1, :] * ubuf_ref[off + t:off + t + ts, :]
    u = acc + cb_ref[...]
    mu = jnp.mean(u, axis=-1, keepdims=True)
    uc = u - mu
    var = jnp.mean(uc * uc, axis=-1, keepdims=True)
    y = uc * lax.rsqrt(var + LN_EPS) * lnw_ref[...] + lnb_ref[...]
    o_ref[0] = (y * _sigmoid(y)).astype(o_ref.dtype)


def _conv(p_b, conv_w, conv_b, ln_w, ln_b, *, ts):
    b, s, cols = p_b.shape
    d_c = cols // 2
    cw = jnp.zeros((CONV_HALO, d_c), F32).at[:CONV_K].set(conv_w.astype(F32))
    row = lambda a: a.reshape(1, d_c).astype(F32)
    par = pl.BlockSpec((1, d_c), lambda i, j: (0, 0))
    body = functools.partial(_conv_body, ts=ts)
    return pl.pallas_call(
        body,
        grid=(b, s // ts),
        in_specs=[pl.BlockSpec((1, ts, d_c), lambda i, j: (i, j, 0)),
                  pl.BlockSpec((1, ts, d_c), lambda i, j: (i, j, 1)),
                  pl.BlockSpec((CONV_HALO, d_c), lambda i, j: (0, 0)), par, par, par],
        out_specs=pl.BlockSpec((1, ts, d_c), lambda i, j: (i, j, 0)),
        out_shape=jax.ShapeDtypeStruct((b, s, d_c), BF16),
        scratch_shapes=[pltpu.VMEM((ts + CONV_HALO, d_c), F32)],
        compiler_params=_params(("arbitrary", "arbitrary"), VMEM_BIG),
        name="conformer_conv",
    )(p_b, p_b, cw, row(conv_b), row(ln_w), row(ln_b))


def _xattn_body(q_ref, k_ref, v_ref, o_ref, *, heads, hd):
    scale = hd ** -0.5
    for h in range(heads):
        ls = slice(h * hd, (h + 1) * hd)
        s = _nt(q_ref[0, :, ls], k_ref[0, :, ls]) * scale
        s = s - jnp.max(s, axis=-1, keepdims=True)
        p = jnp.exp(s)
        p = p / jnp.sum(p, axis=-1, keepdims=True)
        o_ref[0, :, ls] = _mm(p.astype(v_ref.dtype), v_ref[0, :, ls]).astype(o_ref.dtype)


def _xattn(q, k, v, *, ts):
    b, s, d = q.shape
    m = k.shape[1]
    hd = d // XATTN_HEADS
    body = functools.partial(_xattn_body, heads=XATTN_HEADS, hd=hd)
    return pl.pallas_call(
        body,
        grid=(b, s // ts),
        in_specs=[pl.BlockSpec((1, ts, d), lambda i, j: (i, j, 0)),
                  pl.BlockSpec((1, m, d), lambda i, j: (i, 0, 0)),
                  pl.BlockSpec((1, m, d), lambda i, j: (i, 0, 0))],
        out_specs=pl.BlockSpec((1, ts, d), lambda i, j: (i, j, 0)),
        out_shape=jax.ShapeDtypeStruct((b, s, d), BF16),
        compiler_params=_params(("parallel", "parallel"), VMEM_BIG),
        name="xattn",
    )(q, k, v)


def _router_body(x_ref, w_ref, wr_ref, br_ref, xn_ref, lg_ref):
    x = x_ref[...]
    ms = jnp.mean(x * x, axis=-1, keepdims=True)
    xn = x * lax.rsqrt(ms + RMS_EPS) * w_ref[...]
    xn_ref[...] = xn
    lg_ref[...] = _mm(xn, wr_ref[...], HI) + br_ref[...]


def _router(x, norm_w, wr, br, tm=256):
    m, d = x.shape
    tm = min(tm, m)
    return pl.pallas_call(
        _router_body,
        grid=(m // tm,),
        in_specs=[pl.BlockSpec((tm, d), lambda i: (i, 0)), pl.BlockSpec((1, d), lambda i: (0, 0)),
                  pl.BlockSpec((d, LANE), lambda i: (0, 0)), pl.BlockSpec((1, LANE), lambda i: (0, 0))],
        out_specs=[pl.BlockSpec((tm, d), lambda i: (i, 0)), pl.BlockSpec((tm, LANE), lambda i: (i, 0))],
        out_shape=[jax.ShapeDtypeStruct((m, d), F32), jax.ShapeDtypeStruct((m, LANE), F32)],
        compiler_params=_params(("parallel",), VMEM_BIG),
        name="moe_router",
    )(x, norm_w.reshape(1, d).astype(F32), wr, br)


def _moe_body(te_ref, nval_ref, nact_ref, tok0_ref, tokn_ref, dst_ref, wgt_ref, x_hbm, w1_ref, w3_ref,
              w2_ref, out_hbm, xbuf, ybuf, gsem, ssem, *, tmr, n_tiles):
    del te_ref
    i = pl.program_id(0)
    nact = nact_ref[0]
    slot = lax.rem(i, 2)

    def gather_copy(tok, sl, r):
        return pltpu.make_async_copy(x_hbm.at[pl.ds(tok, 1)], xbuf.at[sl, pl.ds(r, 1)], gsem.at[sl])

    def scatter_copy(dst, sl, r):
        return pltpu.make_async_copy(ybuf.at[sl, pl.ds(r, 1)], out_hbm.at[pl.ds(dst, 1)], ssem.at[sl])

    def start_gather(tok_ref, sl):
        def body(r, _):
            gather_copy(tok_ref[0, 0, r], sl, r).start()
            return 0
        lax.fori_loop(0, tmr, body, 0)

    def wait_gather(sl):
        def body(r, _):
            gather_copy(0, sl, r).wait()
            return 0
        lax.fori_loop(0, tmr, body, 0)

    def start_scatter(sl, n):
        def body(r, _):
            scatter_copy(dst_ref[0, 0, r], sl, r).start()
            return 0
        lax.fori_loop(0, n, body, 0)

    def wait_scatter(sl, n):
        def body(r, _):
            scatter_copy(0, sl, r).wait()
            return 0
        lax.fori_loop(0, n, body, 0)

    @pl.when((i == 0) & (nact > 0))
    def _():
        start_gather(tok0_ref, 0)

    @pl.when(i < nact)
    def _():
        @pl.when(i + 1 < nact)
        def _():
            start_gather(tokn_ref, 1 - slot)

        wait_gather(slot)

        @pl.when(i >= 2)
        def _():
            wait_scatter(slot, nval_ref[jnp.maximum(i - 2, 0)])

        x = xbuf[slot].astype(BF16)
        h1 = _mm(x, w1_ref[0])
        h3 = _mm(x, w3_ref[0])
        hh = (h1 * _sigmoid(h1)) * h3 * wgt_ref[0]
        ybuf[slot] = _mm(hh.astype(BF16), w2_ref[0])
        start_scatter(slot, nval_ref[i])

    @pl.when((i == n_tiles - 1) & (nact >= 1))
    def _():
        last = nact - 1
        wait_scatter(lax.rem(last, 2), nval_ref[last])

        @pl.when(nact >= 2)
        def _():
            wait_scatter(lax.rem(last - 1, 2), nval_ref[jnp.maximum(last - 1, 0)])


def _moe_experts(xn, tile_expert, n_valid, n_active, row_tok, row_dst, row_wgt, w1, w3, w2, *, tmr):
    t, d = xn.shape
    n_tiles = tile_expert.shape[0]
    f = w1.shape[2]
    body = functools.partial(_moe_body, tmr=tmr, n_tiles=n_tiles)
    smem_rows = lambda imap: pl.BlockSpec((1, 1, tmr), imap, memory_space=pltpu.SMEM)
    grid_spec = pltpu.PrefetchScalarGridSpec(
        num_scalar_prefetch=3,
        grid=(n_tiles,),
        in_specs=[
            smem_rows(lambda i, te, nv, na: (0, 0, 0)),
            smem_rows(lambda i, te, nv, na: (jnp.minimum(i + 1, n_tiles - 1), 0, 0)),
            smem_rows(lambda i, te, nv, na: (i, 0, 0)),
            pl.BlockSpec((1, tmr, 1), lambda i, te, nv, na: (i, 0, 0)),
            pl.BlockSpec(memory_space=pl.ANY),
            pl.BlockSpec((1, d, f), lambda i, te, nv, na: (te[i], 0, 0)),
            pl.BlockSpec((1, d, f), lambda i, te, nv, na: (te[i], 0, 0)),
            pl.BlockSpec((1, f, d), lambda i, te, nv, na: (te[i], 0, 0)),
        ],
        out_specs=pl.BlockSpec(memory_space=pl.ANY),
        scratch_shapes=[pltpu.VMEM((2, tmr, d), F32), pltpu.VMEM((2, tmr, d), F32),
                        pltpu.SemaphoreType.DMA((2,)), pltpu.SemaphoreType.DMA((2,))],
    )
    tok3 = row_tok.reshape(n_tiles, 1, tmr)
    return pl.pallas_call(
        body,
        grid_spec=grid_spec,
        out_shape=jax.ShapeDtypeStruct((2 * t, d), F32),
        compiler_params=_params(("arbitrary",), VMEM_BIG),
        name="moe_experts",
    )(tile_expert, n_valid, n_active, tok3, tok3, row_dst.reshape(n_tiles, 1, tmr),
      row_wgt.reshape(n_tiles, tmr, 1), xn, w1, w3, w2)


def _moe_route(logits, t, tmr):
    lg1 = logits[:, :N_GROUPS]
    p1 = jax.nn.softmax(lg1, axis=-1)
    grp = jnp.argmax(lg1, axis=-1)
    gate1 = jnp.take_along_axis(p1, grp[:, None], axis=-1)
    lg2_all = logits[:, N_GROUPS:N_GROUPS + N_EXPERTS].reshape(t, N_GROUPS, EXPERTS_PER_GROUP)
    lg2 = jnp.take_along_axis(lg2_all, grp[:, None, None], axis=1)[:, 0]
    p2 = jax.nn.softmax(lg2, axis=-1)
    top_p, top_i = lax.top_k(p2, 2)
    top_p = top_p / jnp.sum(top_p, axis=-1, keepdims=True)
    weights = gate1 * top_p
    eid = (grp[:, None] * EXPERTS_PER_GROUP + top_i).astype(jnp.int32)

    n_assign = 2 * t
    eflat = eid.T.reshape(n_assign)
    wflat = weights.T.reshape(n_assign)
    order = jnp.argsort(eflat, stable=True).astype(jnp.int32)
    counts = jnp.zeros((N_EXPERTS,), jnp.int32).at[eflat].add(1)
    starts = jnp.cumsum(counts) - counts
    tiles_per = (counts + tmr - 1) // tmr
    tile_starts = jnp.cumsum(tiles_per) - tiles_per
    n_active = jnp.sum(tiles_per).astype(jnp.int32)
    n_tiles = (n_assign + N_EXPERTS * (tmr - 1)) // tmr + 1
    n_rows = n_tiles * tmr

    e_sorted = eflat[order]
    pos = jnp.arange(n_assign, dtype=jnp.int32)
    dest_row = tile_starts[e_sorted] * tmr + (pos - starts[e_sorted])
    row_tok = jnp.zeros((n_rows,), jnp.int32).at[dest_row].set(order % t)
    row_dst = jnp.full((n_rows,), n_assign, jnp.int32).at[dest_row].set(order)
    row_wgt = jnp.zeros((n_rows,), F32).at[dest_row].set(wflat[order])
    tile_ids = jnp.arange(n_tiles, dtype=jnp.int32)
    tile_expert = jnp.searchsorted(jnp.cumsum(tiles_per), tile_ids, side="right").astype(jnp.int32)
    tile_expert = jnp.minimum(tile_expert, N_EXPERTS - 1)
    n_valid = jnp.clip(counts[tile_expert] - (tile_ids - tile_starts[tile_expert]) * tmr, 0, tmr)
    n_valid = jnp.where(tile_ids < n_active, n_valid, 0).astype(jnp.int32)
    last_e = tile_expert[jnp.maximum(n_active - 1, 0)]
    tile_expert = jnp.where(tile_ids < n_active, tile_expert, last_e)
    return tile_expert, n_valid, n_active.reshape(1), row_tok, row_dst, row_wgt


def _moe(x, norm_w, wr, br, w1, w3, w2, *, tmr=256):
    t, d = x.shape
    xn, logits = _router(x, norm_w, wr, br)
    tile_expert, n_valid, n_active, row_tok, row_dst, row_wgt = _moe_route(logits, t, tmr)
    out2 = _moe_experts(xn, tile_expert, n_valid, n_active, row_tok, row_dst, row_wgt, w1, w3, w2, tmr=tmr)
    return out2[:t], out2[t:2 * t]


def _pad_cols(a, width):
    return jnp.pad(a, ((0, 0), (0, width - a.shape[1])))


def _pad_rows(a, height):
    return jnp.pad(a, ((0, height - a.shape[0]), (0, 0)))


def kernel(x, mem, norm_mix_w, w_in, rwkv_mu, rwkv_w_up, rwkv_w0, rwkv_a_up, rwkv_a0, rwkv_g_up, rwkv_k_k, rwkv_k_a, rwkv_r_k, rwkv_lnx_w, rwkv_lnx_b, conv_w, conv_b, conv_ln_w, conv_ln_b, conv_pw2, conv_pw2_b, hgrn_lb_logits, hgrn_norm_w, w_out, norm_xattn_w, mem_norm_w, xattn_wq, xattn_wk, xattn_wv, xattn_wo, norm_ffn_w, router_g, router_g_b, router_e, router_e_b, expert_w1, expert_w3, expert_w2, final_norm_w):
    bsz, seq, d = x.shape
    t = bsz * seq
    depth = w_in.shape[0]
    d_r = rwkv_w0.shape[1]
    d_c = conv_b.shape[1]
    d_h = hgrn_norm_w.shape[1]
    dl = rwkv_w_up.shape[1]
    al = rwkv_a_up.shape[1]
    gl = rwkv_g_up.shape[1]
    a_cols = 3 * d_r + dl + al + gl
    b_cols = 2 * d_c
    ts = min(256, seq)

    sm = jax.nn.softmax(hgrn_lb_logits.astype(F32), axis=0)
    lower_bounds = jnp.cumsum(sm, axis=0) - sm[0:1]

    m_mem = mem.shape[1]
    mem_n = _rmsnorm(mem.reshape(bsz * m_mem, d), mem_norm_w, BF16)

    xs = x.reshape(t, d)
    h = _rmsnorm(xs, norm_mix_w[0], BF16)
    out = None
    for l in range(depth):
        wl = w_in[l]
        wa = wl[:, :a_cols]
        c0 = 3 * d_r
        wa_pad = jnp.concatenate([
            wa[:, :c0], _pad_cols(wa[:, c0:c0 + dl], LANE), _pad_cols(wa[:, c0 + dl:c0 + dl + al], LANE),
            _pad_cols(wa[:, c0 + dl + al:], 2 * LANE)], axis=1).astype(BF16)
        wb = wl[:, a_cols:a_cols + b_cols].astype(BF16)
        wc = wl[:, a_cols + b_cols:].astype(BF16)
        p_a = _matmul([h], [wa_pad]).reshape(bsz, seq, -1)
        p_b = _matmul([h], [wb]).reshape(bsz, seq, -1)
        p_c = _matmul([h], [wc]).reshape(bsz, seq, -1)

        mu = rwkv_mu[l]
        mu_pad = jnp.concatenate([
            mu[:c0], jnp.pad(mu[c0:c0 + dl], (0, LANE - dl)), jnp.pad(mu[c0 + dl:c0 + dl + al], (0, LANE - al)),
            jnp.pad(mu[c0 + dl + al:], (0, 2 * LANE - gl))])
        rt, kt, bt, at, v, gc, g, bonus = _rwkv_prep(
            p_a, mu_pad, _pad_rows(rwkv_w_up[l], LANE), rwkv_w0[l], _pad_rows(rwkv_a_up[l], LANE), rwkv_a0[l],
            _pad_rows(rwkv_g_up[l], 2 * LANE), rwkv_k_k[l], rwkv_k_a[l], rwkv_r_k[l].reshape(-1), d_r=d_r, ts=ts)
        y_a = _rwkv_scan(rt, kt, bt, at, v, gc, g, bonus, rwkv_lnx_w[l], rwkv_lnx_b[l], ts=ts)

        u_b = _conv(p_b, conv_w[l], conv_b[l], conv_ln_w[l], conv_ln_b[l], ts=ts)
        y_b = _matmul([u_b.reshape(t, d_c)], [conv_pw2[l].astype(BF16)], bias=conv_pw2_b[l], out_dtype=BF16)

        y_c = _hgrn(p_c, lower_bounds[l], hgrn_norm_w[l], ts=ts)

        wo = w_out[l].astype(BF16)
        xs = _matmul([y_a.reshape(t, d_r), y_b, y_c.reshape(t, d_h)],
                     [wo[:d_r], wo[d_r:d_r + d_c], wo[d_r + d_c:]], residual=xs)

        xn = _rmsnorm(xs, norm_xattn_w[l], BF16)
        q = _matmul([xn], [xattn_wq[l].astype(BF16)], out_dtype=BF16)
        dx = q.shape[1]
        kmem = _matmul([mem_n], [xattn_wk[l].astype(BF16)], out_dtype=BF16)
        vmem = _matmul([mem_n], [xattn_wv[l].astype(BF16)], out_dtype=BF16)
        att = _xattn(q.reshape(bsz, seq, dx), kmem.reshape(bsz, m_mem, dx), vmem.reshape(bsz, m_mem, dx), ts=ts)
        xs = _matmul([att.reshape(t, dx)], [xattn_wo[l].astype(BF16)], residual=xs)

        wr = jnp.concatenate([router_g[l], jnp.transpose(router_e[l], (1, 0, 2)).reshape(d, N_EXPERTS)], axis=1)
        wr = _pad_cols(wr.astype(F32), LANE)
        br = jnp.pad(jnp.concatenate([router_g_b[l], router_e_b[l].reshape(-1)]).astype(F32),
                     (0, LANE - N_GROUPS - N_EXPERTS)).reshape(1, LANE)
        o1, o2 = _moe(xs, norm_ffn_w[l], wr, br, expert_w1[l].astype(BF16), expert_w3[l].astype(BF16),
                      expert_w2[l].astype(BF16))
        if l + 1 < depth:
            xs, h = _add3_rmsnorm(xs, o1, o2, norm_mix_w[l + 1], BF16)
        else:
            _, out = _add3_rmsnorm(xs, o1, o2, final_norm_w, F32)
    return out.reshape(bsz, seq, d)
```
